```python
import jax, jax.numpy as jnp
from jax import lax
import numpy as np

D_MODEL = 1024
BATCH = 4
SEQ = 8192
DEPTH = 2
DEC_BATCH = 8
DEC_SEQ = 16
PAST_LEN = 2048

CHUNK = 64
N_A_LAYERS = max(1, DEPTH // 2)
N_B_LAYERS = DEPTH - N_A_LAYERS
HEAD_DIM = 64
FOX_HEADS = D_MODEL // HEAD_DIM
SWA_Q_HEADS = D_MODEL // HEAD_DIM
SWA_KV_HEADS = SWA_Q_HEADS // 4
SWA_GROUP = SWA_Q_HEADS // SWA_KV_HEADS
WINDOW = 128
WIN_CHUNKS = WINDOW // CHUNK
Q_BLOCK = 128
D_FF = -(-(8 * D_MODEL) // (3 * 256)) * 256
PLE_DIM = 256
EPS = 1e-6
SCALE = HEAD_DIM ** -0.5

kernel_name = 'yoco_fox_swa_sink_streaming_step'


def _rms(x, g):
    x32 = x.astype(jnp.float32)
    y = x32 * lax.rsqrt(jnp.mean(x32 * x32, axis=-1, keepdims=True) + EPS)
    return (y * g.astype(jnp.float32)).astype(x.dtype)


def _swiglu(a, w_gate, w_up, w_down):
    return (jax.nn.silu(a @ w_gate) * (a @ w_up)) @ w_down


def _alibi_slopes():
    n = SWA_Q_HEADS
    s = jnp.exp2(-8.0 * jnp.arange(1, n + 1, dtype=jnp.float32) / n)
    return s.reshape(SWA_KV_HEADS, SWA_GROUP)


def _band_mask_dist(qpos, kpos):
    dc = qpos // CHUNK - kpos // CHUNK
    mask = (dc >= 0) & (dc <= WIN_CHUNKS) & (kpos >= 0)
    dist = jnp.abs(qpos - kpos).astype(jnp.float32)
    return mask, dist


def _sink_softmax(s, sink):
    m = jnp.maximum(jnp.max(s, axis=-1, keepdims=True), sink)
    e = jnp.exp(s - m)
    return e / (jnp.sum(e, axis=-1, keepdims=True) + jnp.exp(sink - m))


def _fox_prompt(q, k, v, logf):
    b, s_len, h, hd = q.shape
    nb = s_len // Q_BLOCK
    F = jnp.cumsum(logf, axis=1)
    Fk = jnp.swapaxes(F, 1, 2)[:, :, None, :]
    qb = jnp.moveaxis(q.reshape(b, nb, Q_BLOCK, h, hd), 1, 0)
    Fb = jnp.moveaxis(F.reshape(b, nb, Q_BLOCK, h), 1, 0)
    kpos = jnp.arange(s_len)

    def block(args):
        qi, Fi, bi = args
        sc = jnp.einsum('bqhd,bkhd->bhqk', qi, k).astype(jnp.float32) * SCALE
        sc = sc + jnp.swapaxes(Fi, 1, 2)[..., None] - Fk
        qpos = bi * Q_BLOCK + jnp.arange(Q_BLOCK)
        sc = jnp.where(kpos[None, :] <= qpos[:, None], sc, -jnp.inf)
        w = jax.nn.softmax(sc, axis=-1)
        return jnp.einsum('bhqk,bkhd->bqhd', w.astype(v.dtype), v)

    out = lax.map(block, (qb, Fb, jnp.arange(nb)))
    return jnp.moveaxis(out, 0, 1).reshape(b, s_len, h, hd)


def _fox_sample(q, k, v, logf, ck, cv, clogf):
    t = q.shape[1]
    p_len = ck.shape[1]
    k_all = jnp.concatenate([ck, k.astype(ck.dtype)], axis=1)
    v_all = jnp.concatenate([cv, v.astype(cv.dtype)], axis=1)
    F = jnp.cumsum(jnp.concatenate([clogf.astype(jnp.float32), logf], axis=1), axis=1)
    Fq = F[:, p_len:]
    sc = jnp.einsum('bthd,bshd->bhts', q, k_all).astype(jnp.float32) * SCALE
    sc = sc + jnp.swapaxes(Fq, 1, 2)[..., None] - jnp.swapaxes(F, 1, 2)[:, :, None, :]
    qpos = p_len + jnp.arange(t)
    kpos = jnp.arange(p_len + t)
    sc = jnp.where(kpos[None, :] <= qpos[:, None], sc, -jnp.inf)
    w = jax.nn.softmax(sc, axis=-1)
    return jnp.einsum('bhts,bshd->bthd', w.astype(v_all.dtype), v_all)


def _swa_prompt(q, k, v, sink, slopes):
    b, s_len, _, hd = q.shape
    nb = s_len // WINDOW
    pad = ((0, 0), (WINDOW, 0), (0, 0), (0, 0))
    kp = jnp.pad(k, pad).reshape(b, nb + 1, WINDOW, SWA_KV_HEADS, hd)
    vp = jnp.pad(v, pad).reshape(b, nb + 1, WINDOW, SWA_KV_HEADS, hd)
    kb = jnp.concatenate([kp[:, :-1], kp[:, 1:]], axis=2)
    vb = jnp.concatenate([vp[:, :-1], vp[:, 1:]], axis=2)
    qg = q.reshape(b, nb, WINDOW, SWA_KV_HEADS, SWA_GROUP, hd)
    blk = jnp.arange(nb)[:, None] * WINDOW
    qpos = blk + jnp.arange(WINDOW)[None, :]
    kpos = blk - WINDOW + jnp.arange(2 * WINDOW)[None, :]
    mask, dist = _band_mask_dist(qpos[:, :, None], kpos[:, None, :])
    sc = jnp.einsum('bnqkgd,bnskd->bnkgqs', qg, kb).astype(jnp.float32) * SCALE
    sc = sc - slopes[None, None, :, :, None, None] * dist[None, :, None, None]
    sc = jnp.where(mask[None, :, None, None], sc, -jnp.inf)
    sk = sink.astype(jnp.float32).reshape(SWA_KV_HEADS, SWA_GROUP)[None, None, :, :, None, None]
    w = _sink_softmax(sc, sk)
    o = jnp.einsum('bnkgqs,bnskd->bnqkgd', w.astype(vb.dtype), vb)
    return o.reshape(b, s_len, SWA_Q_HEADS, hd)


def _swa_sample(q, k_all, v_all, sink, slopes, past_len):
    b, t, _, hd = q.shape
    l = k_all.shape[1]
    qpos = past_len + jnp.arange(t)
    kpos = past_len + t - l + jnp.arange(l)
    mask, dist = _band_mask_dist(qpos[:, None], kpos[None, :])
    qg = q.reshape(b, t, SWA_KV_HEADS, SWA_GROUP, hd)
    sc = jnp.einsum('btkgd,bskd->bkgts', qg, k_all).astype(jnp.float32) * SCALE
    sc = sc - slopes[None, :, :, None, None] * dist
    sc = jnp.where(mask, sc, -jnp.inf)
    sk = sink.astype(jnp.float32).reshape(SWA_KV_HEADS, SWA_GROUP)[None, :, :, None, None]
    w = _sink_softmax(sc, sk)
    o = jnp.einsum('bkgts,bskd->btkgd', w.astype(v_all.dtype), v_all)
    return o.reshape(b, t, SWA_Q_HEADS, hd)


def _trunk(x, p, fox_attend, swa_attend, norm_mix_pre, norm_mix_post, norm_ffn_pre,
           norm_ffn_post, fox_w_in, fox_b_f, fox_w_out, swa_w_q, swa_sinks, swa_w_out,
           kv_norm, swa_w_kv, ffn_w_gate, ffn_w_up, ffn_w_down, ple_norm, ple_w_gate,
           ple_b_gate, ple_w_proj):
    b, t, _ = x.shape
    n_qkv = 3 * FOX_HEADS * HEAD_DIM
    h = x
    fox_states = []
    kv_k = None
    kv_v = None
    for i in range(DEPTH):
        a = _rms(h, norm_mix_pre[i])
        if i < N_A_LAYERS:
            proj = a @ fox_w_in[i]
            qkv = proj[..., :n_qkv].reshape(b, t, 3, FOX_HEADS, HEAD_DIM)
            q, k, v = qkv[:, :, 0], qkv[:, :, 1], qkv[:, :, 2]
            logf = jax.nn.log_sigmoid(proj[..., n_qkv:].astype(jnp.float32)
                                      + fox_b_f[i].astype(jnp.float32))
            o = fox_attend(i, q, k, v, logf)
            fox_states.append((k, v, logf))
            mix = o.reshape(b, t, FOX_HEADS * HEAD_DIM) @ fox_w_out[i]
        else:
            j = i - N_A_LAYERS
            q = (a @ swa_w_q[j]).reshape(b, t, SWA_Q_HEADS, HEAD_DIM)
            o = swa_attend(q, kv_k, kv_v, swa_sinks[j])
            mix = o.reshape(b, t, SWA_Q_HEADS * HEAD_DIM) @ swa_w_out[j]
        h = h + _rms(mix, norm_mix_post[i])
        f = _swiglu(_rms(h, norm_ffn_pre[i]), ffn_w_gate[i], ffn_w_up[i], ffn_w_down[i])
        h = h + _rms(f, norm_ffn_post[i])
        gate = jax.nn.sigmoid(_rms(h, ple_norm[i]) @ ple_w_gate[i] + ple_b_gate[i])
        h = h + gate * (p[i] @ ple_w_proj[i])
        if i == N_A_LAYERS - 1:
            kv = (_rms(h, kv_norm) @ swa_w_kv).reshape(b, t, 2, SWA_KV_HEADS, HEAD_DIM)
            kv_k, kv_v = kv[:, :, 0], kv[:, :, 1]
    return h, fox_states, kv_k, kv_v


def setup_inputs(seed: int = 0) -> dict:
    key = jax.random.key(seed)
    it = iter(jax.random.split(key, 40))
    f32 = jnp.float32
    D = D_MODEL

    def nrm(shape, scale=1.0):
        return jax.random.normal(next(it), shape, f32) * scale

    def gain(shape):
        return 1.0 + 0.02 * nrm(shape)

    swa_buf = min(WINDOW, PAST_LEN)
    return {
        'x_prompt': nrm((BATCH, SEQ, D)),
        'x_sample': nrm((DEC_BATCH, DEC_SEQ, D)),
        'cache_fox_k': nrm((N_A_LAYERS, DEC_BATCH, PAST_LEN, FOX_HEADS, HEAD_DIM)),
        'cache_fox_v': nrm((N_A_LAYERS, DEC_BATCH, PAST_LEN, FOX_HEADS, HEAD_DIM)),
        'cache_fox_logf': jax.nn.log_sigmoid(4.0 + nrm((N_A_LAYERS, DEC_BATCH, PAST_LEN, FOX_HEADS))),
        'cache_swa_k': nrm((DEC_BATCH, swa_buf, SWA_KV_HEADS, HEAD_DIM)),
        'cache_swa_v': nrm((DEC_BATCH, swa_buf, SWA_KV_HEADS, HEAD_DIM)),
        'p_prompt': nrm((DEPTH, BATCH, SEQ, PLE_DIM)),
        'p_sample': nrm((DEPTH, DEC_BATCH, DEC_SEQ, PLE_DIM)),
        'norm_mix_pre': gain((DEPTH, D)),
        'norm_mix_post': gain((DEPTH, D)),
        'norm_ffn_pre': gain((DEPTH, D)),
        'norm_ffn_post': gain((DEPTH, D)),
        'fox_w_in': nrm((N_A_LAYERS, D, 3 * FOX_HEADS * HEAD_DIM + FOX_HEADS), D ** -0.5),
        'fox_b_f': jnp.linspace(1.0, 6.0, FOX_HEADS, dtype=f32)[None, :] + 0.01 * nrm((N_A_LAYERS, FOX_HEADS)),
        'fox_w_out': nrm((N_A_LAYERS, FOX_HEADS * HEAD_DIM, D), (FOX_HEADS * HEAD_DIM) ** -0.5),
        'swa_w_q': nrm((N_B_LAYERS, D, SWA_Q_HEADS * HEAD_DIM), D ** -0.5),
        'swa_sinks': nrm((N_B_LAYERS, SWA_Q_HEADS), 0.5),
        'swa_w_out': nrm((N_B_LAYERS, SWA_Q_HEADS * HEAD_DIM, D), (SWA_Q_HEADS * HEAD_DIM) ** -0.5),
        'kv_norm': gain((D,)),
        'swa_w_kv': nrm((D, 2 * SWA_KV_HEADS * HEAD_DIM), D ** -0.5),
        'ffn_w_gate': nrm((DEPTH, D, D_FF), D ** -0.5),
        'ffn_w_up': nrm((DEPTH, D, D_FF), D ** -0.5),
        'ffn_w_down': nrm((DEPTH, D_FF, D), D_FF ** -0.5),
        'ple_norm': gain((DEPTH, D)),
        'ple_w_gate': nrm((DEPTH, D, D), D ** -0.5),
        'ple_b_gate': nrm((DEPTH, D), 0.02),
        'ple_w_proj': nrm((DEPTH, PLE_DIM, D), PLE_DIM ** -0.5),
    }


def reference(x_prompt, x_sample, cache_fox_k, cache_fox_v, cache_fox_logf, cache_swa_k,
              cache_swa_v, p_prompt, p_sample, norm_mix_pre, norm_mix_post, norm_ffn_pre,
              norm_ffn_post, fox_w_in, fox_b_f, fox_w_out, swa_w_q, swa_sinks, swa_w_out,
              kv_norm, swa_w_kv, ffn_w_gate, ffn_w_up, ffn_w_down, ple_norm, ple_w_gate,
              ple_b_gate, ple_w_proj):
    weights = (norm_mix_pre, norm_mix_post, norm_ffn_pre, norm_ffn_post, fox_w_in, fox_b_f,
               fox_w_out, swa_w_q, swa_sinks, swa_w_out, kv_norm, swa_w_kv, ffn_w_gate,
               ffn_w_up, ffn_w_down, ple_norm, ple_w_gate, ple_b_gate, ple_w_proj)
    slopes = _alibi_slopes()
    past_len = cache_fox_k.shape[2]

    def fox_p(i, q, k, v, logf):
        return _fox_prompt(q, k, v, logf)

    def swa_p(q, k, v, sink):
        return _swa_prompt(q, k, v, sink, slopes)

    y_prompt, fox_p_states, kv_k_p, kv_v_p = _trunk(x_prompt, p_prompt, fox_p, swa_p, *weights)

    def fox_s(i, q, k, v, logf):
        return _fox_sample(q, k, v, logf, cache_fox_k[i], cache_fox_v[i], cache_fox_logf[i])

    def swa_s(q, k, v, sink):
        k_all = jnp.concatenate([cache_swa_k, k.astype(cache_swa_k.dtype)], axis=1)
        v_all = jnp.concatenate([cache_swa_v, v.astype(cache_swa_v.dtype)], axis=1)
        return _swa_sample(q, k_all, v_all, sink, slopes, past_len)

    y_sample, fox_s_states, kv_k_s, kv_v_s = _trunk(x_sample, p_sample, fox_s, swa_s, *weights)

    fox_k_prompt = jnp.stack([st[0] for st in fox_p_states])
    fox_v_prompt = jnp.stack([st[1] for st in fox_p_states])
    fox_logf_prompt = jnp.stack([st[2] for st in fox_p_states])
    fox_k_sample = jnp.stack([st[0] for st in fox_s_states])
    fox_v_sample = jnp.stack([st[1] for st in fox_s_states])
    fox_logf_sample = jnp.stack([st[2] for st in fox_s_states])
    buf_p = min(WINDOW, x_prompt.shape[1])
    swa_k_prompt = kv_k_p[:, -buf_p:]
    swa_v_prompt = kv_v_p[:, -buf_p:]
    buf_s = cache_swa_k.shape[1]
    swa_k_sample = jnp.concatenate([cache_swa_k, kv_k_s.astype(cache_swa_k.dtype)], axis=1)[:, -buf_s:]
    swa_v_sample = jnp.concatenate([cache_swa_v, kv_v_s.astype(cache_swa_v.dtype)], axis=1)[:, -buf_s:]
    return (y_prompt, y_sample, fox_k_prompt, fox_v_prompt, fox_logf_prompt, fox_k_sample,
            fox_v_sample, fox_logf_sample, swa_k_prompt, swa_v_prompt, swa_k_sample, swa_v_sample)
```

```python
import functools

import numpy as np
import jax
import jax.numpy as jnp
from jax import lax
from jax.experimental import pallas as pl
from jax.experimental.pallas import tpu as pltpu

F32 = jnp.float32
BF16 = jnp.bfloat16

D_MODEL = 1024
HEAD_DIM = 64
N_HEADS = D_MODEL // HEAD_DIM
SWA_KV_HEADS = 4
SWA_GROUP = N_HEADS // SWA_KV_HEADS
CHUNK = 64
WIN_CHUNKS = 2
WINDOW = 128
EPS = 1e-6
SCALE = HEAD_DIM ** -0.5
LANES = 128
PAIRS = D_MODEL // LANES
NEG_BIG = -1e30
MASK_DIST = 1e30
VMEM_LIMIT = 56 * 1024 * 1024

SWA_PERM = np.array([8 * c + 4 * hh + g for c in range(2) for g in range(4) for hh in range(2)])
SWA_SLOPES = [float(2.0 ** (-8.0 * (h + 1) / N_HEADS)) for h in SWA_PERM]


def _rms(x, g):
    y = x * lax.rsqrt(jnp.mean(x * x, axis=-1, keepdims=True) + EPS)
    return y * g


def _log_sigmoid(x):
    return jnp.minimum(x, 0.0) - jnp.log1p(jnp.exp(-jnp.abs(x)))


def _dot(a, b):
    return jnp.dot(a, b, preferred_element_type=F32)


def _dot_nt(a, b):
    return lax.dot_general(a, b, (((1,), (1,)), ((), ())), preferred_element_type=F32)


def _row_spec(tm, width):
    return pl.BlockSpec((tm, width), lambda i: (i, 0))


def _const_spec(shape):
    return pl.BlockSpec(shape, lambda *_: (0,) * len(shape))


def _params(sem):
    return pltpu.CompilerParams(dimension_semantics=sem, vmem_limit_bytes=VMEM_LIMIT)


def _fox_in_kernel(x_ref, g_ref, wqkv_ref, wf_ref, bf_ref,
                   q_ref, k_ref, v_ref, kb_ref, vb_ref, lf_ref):
    a = _rms(x_ref[...], g_ref[...]).astype(BF16)
    q = _dot(a, wqkv_ref[:, 0:D_MODEL])
    q_ref[...] = (q * SCALE).astype(BF16)
    k = _dot(a, wqkv_ref[:, D_MODEL:2 * D_MODEL])
    k_ref[...] = k
    kb_ref[...] = k.astype(BF16)
    v = _dot(a, wqkv_ref[:, 2 * D_MODEL:3 * D_MODEL])
    v_ref[...] = v
    vb_ref[...] = v.astype(BF16)
    f = _dot(a, wf_ref[...]) + bf_ref[...]
    lf_ref[...] = _log_sigmoid(f[:, 0:N_HEADS])


def _fox_in(x, g, wqkv, wf, bf, tm):
    n = x.shape[0]
    row = lambda w: _row_spec(tm, w)
    return pl.pallas_call(
        _fox_in_kernel,
        grid=(n // tm,),
        in_specs=[row(D_MODEL), _const_spec((1, D_MODEL)), _const_spec(wqkv.shape),
                  _const_spec(wf.shape), _const_spec(bf.shape)],
        out_specs=[row(D_MODEL)] * 5 + [row(N_HEADS)],
        out_shape=[jax.ShapeDtypeStruct((n, D_MODEL), BF16),
                   jax.ShapeDtypeStruct((n, D_MODEL), F32),
                   jax.ShapeDtypeStruct((n, D_MODEL), F32),
                   jax.ShapeDtypeStruct((n, D_MODEL), BF16),
                   jax.ShapeDtypeStruct((n, D_MODEL), BF16),
                   jax.ShapeDtypeStruct((n, N_HEADS), F32)],
        compiler_params=_params(("parallel",)),
        name="fox_in",
    )(x, g, wqkv, wf, bf)


def _cumsum_kernel(lf_ref, f_ref, carry_ref):
    ts = lf_ref.shape[1]

    @pl.when(pl.program_id(1) == 0)
    def _():
        carry_ref[...] = jnp.zeros_like(carry_ref)

    r = lax.broadcasted_iota(jnp.int32, (ts, ts), 0)
    c = lax.broadcasted_iota(jnp.int32, (ts, ts), 1)
    tril = jnp.where(c <= r, 1.0, 0.0).astype(F32)
    cs = jnp.dot(tril, lf_ref[0], preferred_element_type=F32,
                 precision=lax.Precision.HIGHEST) + carry_ref[...]
    f_ref[0] = cs
    carry_ref[...] = cs[ts - 1:ts, :]


def _cumsum(lf, ts):
    b, s, h = lf.shape
    spec = pl.BlockSpec((1, ts, h), lambda bi, i: (bi, i, 0))
    return pl.pallas_call(
        _cumsum_kernel,
        grid=(b, s // ts),
        in_specs=[spec],
        out_specs=spec,
        out_shape=jax.ShapeDtypeStruct((b, s, h), F32),
        scratch_shapes=[pltpu.VMEM((1, h), F32)],
        compiler_params=_params(("parallel", "arbitrary")),
        name="logf_cumsum",
    )(lf)


def _pair_layouts(f):
    b, l, _ = f.shape
    f4 = f.reshape(b, l, PAIRS, 2)
    return jnp.transpose(f4, (0, 2, 1, 3)), jnp.transpose(f4, (0, 2, 3, 1))


def _head_masks(shape):
    lane = lax.broadcasted_iota(jnp.int32, shape, 1)
    return lane < HEAD_DIM


def _fox_kernel(q_ref, k_ref, v_ref, fq_ref, ft_ref, o_ref, m_ref, l_ref, acc_ref, *, tq):
    i = pl.program_id(2)
    q = q_ref[0]
    first = _head_masks(q.shape)
    zero = jnp.zeros_like(q)
    qh = (jnp.where(first, q, zero), jnp.where(first, zero, q))
    fq = (fq_ref[0, 0, :, 0:1], fq_ref[0, 0, :, 1:2])

    def logits(h, start):
        kblk = k_ref[0, pl.ds(start, tq), :]
        fk = ft_ref[0, 0, h:h + 1, pl.ds(start, tq)]
        return _dot_nt(qh[h], kblk) + fq[h] - fk

    d0 = pl.multiple_of(i * tq, tq)
    r = lax.broadcasted_iota(jnp.int32, (tq, tq), 0)
    c = lax.broadcasted_iota(jnp.int32, (tq, tq), 1)
    causal = c <= r
    vd = v_ref[0, pl.ds(d0, tq), :]
    for h in range(2):
        s = jnp.where(causal, logits(h, d0), NEG_BIG)
        m = jnp.max(s, axis=1, keepdims=True)
        p = jnp.exp(s - m)
        m_ref[h] = m
        l_ref[h] = jnp.sum(p, axis=1, keepdims=True)
        acc_ref[h] = _dot(p.astype(BF16), vd)

    def body(j, carry):
        start = pl.multiple_of(j * tq, tq)
        vblk = v_ref[0, pl.ds(start, tq), :]
        for h in range(2):
            s = logits(h, start)
            m_prev = m_ref[h]
            m_new = jnp.maximum(m_prev, jnp.max(s, axis=1, keepdims=True))
            alpha = jnp.exp(m_prev - m_new)
            p = jnp.exp(s - m_new)
            m_ref[h] = m_new
            l_ref[h] = alpha * l_ref[h] + jnp.sum(p, axis=1, keepdims=True)
            acc_ref[h] = alpha * acc_ref[h] + _dot(p.astype(BF16), vblk)
        return carry

    lax.fori_loop(0, i, body, 0)

    o0 = acc_ref[0] / l_ref[0]
    o1 = acc_ref[1] / l_ref[1]
    o_ref[0] = jnp.where(first, o0, o1).astype(o_ref.dtype)


def _fox_prompt(q, kb, vb, fq4, ft4, tq):
    b, s, _ = q.shape
    kv_spec = pl.BlockSpec((1, s, LANES), lambda bi, hp, i: (bi, 0, hp))
    qo_spec = pl.BlockSpec((1, tq, LANES), lambda bi, hp, i: (bi, i, hp))
    return pl.pallas_call(
        functools.partial(_fox_kernel, tq=tq),
        grid=(b, PAIRS, s // tq),
        in_specs=[qo_spec, kv_spec, kv_spec,
                  pl.BlockSpec((1, 1, tq, 2), lambda bi, hp, i: (bi, hp, i, 0)),
                  pl.BlockSpec((1, 1, 2, s), lambda bi, hp, i: (bi, hp, 0, 0))],
        out_specs=qo_spec,
        out_shape=jax.ShapeDtypeStruct((b, s, D_MODEL), BF16),
        scratch_shapes=[pltpu.VMEM((2, tq, 1), F32), pltpu.VMEM((2, tq, 1), F32),
                        pltpu.VMEM((2, tq, LANES), F32)],
        compiler_params=_params(("parallel", "parallel", "arbitrary")),
        name="fox_attention",
    )(q, kb, vb, fq4, ft4)


def _fox_sample_kernel(q_ref, kn_ref, vn_ref, ck_ref, cv_ref, fq_ref, ft_ref, o_ref):
    t = q_ref.shape[1]
    past = ck_ref.shape[1]
    q = q_ref[0]
    first = _head_masks(q.shape)
    zero = jnp.zeros_like(q)
    qh = (jnp.where(first, q, zero), jnp.where(first, zero, q))
    ck = ck_ref[0].astype(BF16)
    cv = cv_ref[0].astype(BF16)
    kn = kn_ref[0]
    vn = vn_ref[0]
    r = lax.broadcasted_iota(jnp.int32, (t, t), 0)
    c = lax.broadcasted_iota(jnp.int32, (t, t), 1)
    outs = []
    for h in range(2):
        fq = fq_ref[0, 0, :, h:h + 1]
        s_c = _dot_nt(qh[h], ck) + fq - ft_ref[0, 0, h:h + 1, 0:past]
        s_n = _dot_nt(qh[h], kn) + fq - ft_ref[0, 0, h:h + 1, past:past + t]
        s_n = jnp.where(c <= r, s_n, NEG_BIG)
        m = jnp.maximum(jnp.max(s_c, axis=1, keepdims=True), jnp.max(s_n, axis=1, keepdims=True))
        p_c = jnp.exp(s_c - m)
        p_n = jnp.exp(s_n - m)
        l = jnp.sum(p_c, axis=1, keepdims=True) + jnp.sum(p_n, axis=1, keepdims=True)
        o = _dot(p_c.astype(BF16), cv) + _dot(p_n.astype(BF16), vn)
        outs.append(o / l)
    o_ref[0] = jnp.where(first, outs[0], outs[1]).astype(o_ref.dtype)


def _fox_sample(q, kb, vb, ck, cv, fq4, ft4):
    b, t, _ = q.shape
    past = ck.shape[1]
    new_spec = pl.BlockSpec((1, t, LANES), lambda bi, hp: (bi, 0, hp))
    cache_spec = pl.BlockSpec((1, past, LANES), lambda bi, hp: (bi, 0, hp))
    return pl.pallas_call(
        _fox_sample_kernel,
        grid=(b, PAIRS),
        in_specs=[new_spec, new_spec, new_spec, cache_spec, cache_spec,
                  pl.BlockSpec((1, 1, t, 2), lambda bi, hp: (bi, hp, 0, 0)),
                  pl.BlockSpec((1, 1, 2, past + t), lambda bi, hp: (bi, hp, 0, 0))],
        out_specs=new_spec,
        out_shape=jax.ShapeDtypeStruct((b, t, D_MODEL), BF16),
        compiler_params=_params(("parallel", "parallel")),
        name="fox_attention_sample",
    )(q, kb, vb, ck, cv, fq4, ft4)


def _swa_kernel(sink_ref, q_ref, k_ref, v_ref, o_ref, *, tq, win, sample, past, slopes):
    if sample:
        kstart = 0
        qpos0 = past
        kpos0 = past + tq - win
    else:
        i = pl.program_id(1)
        kstart = pl.multiple_of(jnp.maximum(i * tq - WINDOW, 0), WINDOW)
        qpos0 = i * tq
        kpos0 = kstart
    qpos = qpos0 + lax.broadcasted_iota(jnp.int32, (tq, win), 0)
    kpos = kpos0 + lax.broadcasted_iota(jnp.int32, (tq, win), 1)
    dc = lax.shift_right_arithmetic(qpos, 6) - lax.shift_right_arithmetic(kpos, 6)
    dist = jnp.abs(qpos - kpos).astype(F32)
    dist = jnp.where(dc >= 0, dist, MASK_DIST)
    dist = jnp.where(dc <= WIN_CHUNKS, dist, MASK_DIST)
    dist = jnp.where(kpos >= 0, dist, MASK_DIST)
    first = _head_masks((tq, LANES))
    for c in range(2):
        kblk = k_ref[0, pl.ds(kstart, win), c * LANES:(c + 1) * LANES]
        vblk = v_ref[0, pl.ds(kstart, win), c * LANES:(c + 1) * LANES]
        for g in range(SWA_GROUP):
            pair = SWA_GROUP * c + g
            q = q_ref[0, :, pair * LANES:(pair + 1) * LANES]
            zero = jnp.zeros_like(q)
            outs = []
            for hh in range(2):
                head = 2 * pair + hh
                qm = jnp.where(first, q, zero) if hh == 0 else jnp.where(first, zero, q)
                s = _dot_nt(qm, kblk) - slopes[head] * dist
                sink = sink_ref[head]
                m = jnp.maximum(jnp.max(s, axis=1, keepdims=True), sink)
                e = jnp.exp(s - m)
                den = jnp.sum(e, axis=1, keepdims=True) + jnp.exp(sink - m)
                outs.append(_dot(e.astype(BF16), vblk) / den)
            o_ref[0, :, pair * LANES:(pair + 1) * LANES] = (
                jnp.where(first, outs[0], outs[1]).astype(o_ref.dtype))


def _swa(sinks, q, k, v, tq, win, sample, past=0, v_block=0):
    b, s, _ = q.shape
    ks = k.shape[1]
    q_spec = pl.BlockSpec((1, tq, D_MODEL), lambda bi, i: (bi, i, 0))
    k_spec = pl.BlockSpec((1, ks, 2 * LANES), lambda bi, i: (bi, 0, 0))
    v_spec = pl.BlockSpec((1, ks, 2 * LANES), lambda bi, i: (bi, 0, v_block))
    return pl.pallas_call(
        functools.partial(_swa_kernel, tq=tq, win=win, sample=sample, past=past,
                          slopes=SWA_SLOPES),
        grid=(b, s // tq),
        in_specs=[pl.BlockSpec(memory_space=pltpu.SMEM), q_spec, k_spec, v_spec],
        out_specs=q_spec,
        out_shape=jax.ShapeDtypeStruct((b, s, D_MODEL), BF16),
        compiler_params=_params(("parallel", "arbitrary")),
        name="swa_attention_sample" if sample else "swa_attention",
    )(sinks, q, k, v)


def _mix_out_kernel(o_ref, h_ref, w_ref, g_ref, out_ref):
    mix = _dot(o_ref[...], w_ref[...])
    out_ref[...] = h_ref[...] + _rms(mix, g_ref[...])


def _mix_out(o, h, w, g, tm):
    n = h.shape[0]
    return pl.pallas_call(
        _mix_out_kernel,
        grid=(n // tm,),
        in_specs=[_row_spec(tm, D_MODEL), _row_spec(tm, D_MODEL), _const_spec(w.shape),
                  _const_spec((1, D_MODEL))],
        out_specs=_row_spec(tm, D_MODEL),
        out_shape=jax.ShapeDtypeStruct((n, D_MODEL), F32),
        compiler_params=_params(("parallel",)),
        name="mix_out",
    )(o, h, w, g)


def _ffn_kernel(h_ref, gpre_ref, wg_ref, wu_ref, wd_ref, gpost_ref, out_ref):
    h = h_ref[...]
    a = _rms(h, gpre_ref[...]).astype(BF16)
    gate = _dot(a, wg_ref[...])
    up = _dot(a, wu_ref[...])
    act = (gate * jax.nn.sigmoid(gate) * up).astype(BF16)
    f = _dot(act, wd_ref[...])
    out_ref[...] = h + _rms(f, gpost_ref[...])


def _ffn(h, gpre, wg, wu, wd, gpost, tm):
    n = h.shape[0]
    vec = _const_spec((1, D_MODEL))
    return pl.pallas_call(
        _ffn_kernel,
        grid=(n // tm,),
        in_specs=[_row_spec(tm, D_MODEL), vec, _const_spec(wg.shape), _const_spec(wu.shape),
                  _const_spec(wd.shape), vec],
        out_specs=_row_spec(tm, D_MODEL),
        out_shape=jax.ShapeDtypeStruct((n, D_MODEL), F32),
        compiler_params=_params(("parallel",)),
        name="ffn",
    )(h, gpre, wg, wu, wd, gpost)


def _ple_update(h_ref, p_ref, gn_ref, wgate_ref, bgate_ref, wproj_ref):
    h = h_ref[...]
    gate = jax.nn.sigmoid(_dot(_rms(h, gn_ref[...]).astype(BF16), wgate_ref[...]) + bgate_ref[...])
    emb = _dot(p_ref[...].astype(BF16), wproj_ref[...])
    return h + gate * emb


def _ple_kernel(h_ref, p_ref, gn_ref, wgate_ref, bgate_ref, wproj_ref, out_ref):
    out_ref[...] = _ple_update(h_ref, p_ref, gn_ref, wgate_ref, bgate_ref, wproj_ref)


def _ple_kv_q_kernel(h_ref, p_ref, gn_ref, wgate_ref, bgate_ref, wproj_ref,
                     gkv_ref, wkv_ref, gq_ref, wq_ref, out_ref, kv_ref, kvb_ref, q_ref):
    h = _ple_update(h_ref, p_ref, gn_ref, wgate_ref, bgate_ref, wproj_ref)
    out_ref[...] = h
    kv = _dot(_rms(h, gkv_ref[...]).astype(BF16), wkv_ref[...])
    kv_ref[...] = kv
    kvb_ref[...] = kv.astype(BF16)
    q = _dot(_rms(h, gq_ref[...]).astype(BF16), wq_ref[...])
    q_ref[...] = (q * SCALE).astype(BF16)


def _ple(h, p, gn, wgate, bgate, wproj, tm, extra=None):
    n = h.shape[0]
    vec = _const_spec((1, D_MODEL))
    in_specs = [_row_spec(tm, D_MODEL), _row_spec(tm, p.shape[1]), vec,
                _const_spec(wgate.shape), vec, _const_spec(wproj.shape)]
    args = [h, p, gn, wgate, bgate, wproj]
    if extra is None:
        return pl.pallas_call(
            _ple_kernel,
            grid=(n // tm,),
            in_specs=in_specs,
            out_specs=_row_spec(tm, D_MODEL),
            out_shape=jax.ShapeDtypeStruct((n, D_MODEL), F32),
            compiler_params=_params(("parallel",)),
            name="ple",
        )(*args)
    gkv, wkv, gq, wq = extra
    kvw = wkv.shape[1]
    return pl.pallas_call(
        _ple_kv_q_kernel,
        grid=(n // tm,),
        in_specs=in_specs + [vec, _const_spec(wkv.shape), vec, _const_spec(wq.shape)],
        out_specs=[_row_spec(tm, D_MODEL), _row_spec(tm, kvw), _row_spec(tm, kvw),
                   _row_spec(tm, D_MODEL)],
        out_shape=[jax.ShapeDtypeStruct((n, D_MODEL), F32),
                   jax.ShapeDtypeStruct((n, kvw), F32),
                   jax.ShapeDtypeStruct((n, kvw), BF16),
                   jax.ShapeDtypeStruct((n, D_MODEL), BF16)],
        compiler_params=_params(("parallel",)),
        name="ple_kv_q",
    )(*args, gkv, wkv, gq, wq)


def _trunk(x, p, w, *, tm, fox_attend, swa_attend):
    b, t, _ = x.shape
    n = b * t
    h = x.reshape(n, D_MODEL)
    p = p.reshape(2, n, -1)

    q, k, v, kb, vb, lf = _fox_in(h, w["g_pre0"], w["wqkv"], w["wf"], w["bf"], tm)
    o = fox_attend(q.reshape(b, t, D_MODEL), kb.reshape(b, t, D_MODEL),
                   vb.reshape(b, t, D_MODEL), lf.reshape(b, t, N_HEADS))
    h = _mix_out(o.reshape(n, D_MODEL), h, w["fox_wout"], w["g_post0"], tm)
    h = _ffn(h, w["g_fpre0"], w["wg0"], w["wu0"], w["wd0"], w["g_fpost0"], tm)
    h, kv, kvb, q1 = _ple(h, p[0], w["g_ple0"], w["wpg0"], w["bpg0"], w["wpp0"], tm,
                          extra=(w["g_kv"], w["wkv"], w["g_pre1"], w["swa_wq"]))

    o = swa_attend(q1.reshape(b, t, D_MODEL), kvb.reshape(b, t, -1))
    h = _mix_out(o.reshape(n, D_MODEL), h, w["swa_wout"], w["g_post1"], tm)
    h = _ffn(h, w["g_fpre1"], w["wg1"], w["wu1"], w["wd1"], w["g_fpost1"], tm)
    h = _ple(h, p[1], w["g_ple1"], w["wpg1"], w["bpg1"], w["wpp1"], tm)

    kv4 = kv.reshape(b, t, 2, SWA_KV_HEADS, HEAD_DIM)
    return (h.reshape(b, t, D_MODEL), k.reshape(b, t, N_HEADS, HEAD_DIM),
            v.reshape(b, t, N_HEADS, HEAD_DIM), lf.reshape(b, t, N_HEADS),
            kv4[:, :, 0], kv4[:, :, 1])


def kernel(x_prompt, x_sample, cache_fox_k, cache_fox_v, cache_fox_logf, cache_swa_k,
           cache_swa_v, p_prompt, p_sample, norm_mix_pre, norm_mix_post, norm_ffn_pre,
           norm_ffn_post, fox_w_in, fox_b_f, fox_w_out, swa_w_q, swa_sinks, swa_w_out,
           kv_norm, swa_w_kv, ffn_w_gate, ffn_w_up, ffn_w_down, ple_norm, ple_w_gate,
           ple_b_gate, ple_w_proj):
    assert norm_mix_pre.shape[0] == 2 and fox_w_in.shape[0] == 1 and swa_w_q.shape[0] == 1
    vec = lambda a: a.reshape(1, -1).astype(F32)
    nqkv = 3 * D_MODEL
    wf = jnp.zeros((D_MODEL, LANES), F32).at[:, :N_HEADS].set(fox_w_in[0][:, nqkv:])
    bf = jnp.zeros((1, LANES), F32).at[:, :N_HEADS].set(fox_b_f[0][None, :])
    wq = swa_w_q[0].reshape(D_MODEL, N_HEADS, HEAD_DIM)[:, SWA_PERM].reshape(D_MODEL, D_MODEL)
    wo = swa_w_out[0].reshape(N_HEADS, HEAD_DIM, D_MODEL)[SWA_PERM].reshape(D_MODEL, D_MODEL)
    sinks = swa_sinks[0][SWA_PERM].astype(F32)
    w = {
        "g_pre0": vec(norm_mix_pre[0]), "g_pre1": vec(norm_mix_pre[1]),
        "g_post0": vec(norm_mix_post[0]), "g_post1": vec(norm_mix_post[1]),
        "g_fpre0": vec(norm_ffn_pre[0]), "g_fpre1": vec(norm_ffn_pre[1]),
        "g_fpost0": vec(norm_ffn_post[0]), "g_fpost1": vec(norm_ffn_post[1]),
        "g_ple0": vec(ple_norm[0]), "g_ple1": vec(ple_norm[1]), "g_kv": vec(kv_norm),
        "wqkv": fox_w_in[0][:, :nqkv].astype(BF16), "wf": wf.astype(BF16), "bf": bf,
        "fox_wout": fox_w_out[0].astype(BF16),
        "swa_wq": wq.astype(BF16), "swa_wout": wo.astype(BF16),
        "wkv": swa_w_kv.astype(BF16),
        "wg0": ffn_w_gate[0].astype(BF16), "wg1": ffn_w_gate[1].astype(BF16),
        "wu0": ffn_w_up[0].astype(BF16), "wu1": ffn_w_up[1].astype(BF16),
        "wd0": ffn_w_down[0].astype(BF16), "wd1": ffn_w_down[1].astype(BF16),
        "wpg0": ple_w_gate[0].astype(BF16), "wpg1": ple_w_gate[1].astype(BF16),
        "bpg0": vec(ple_b_gate[0]), "bpg1": vec(ple_b_gate[1]),
        "wpp0": ple_w_proj[0].astype(BF16), "wpp1": ple_w_proj[1].astype(BF16),
    }

    def fox_p(q, kb, vb, lf):
        fq4, ft4 = _pair_layouts(_cumsum(lf, 512))
        return _fox_prompt(q, kb, vb, fq4, ft4, 256)

    def swa_p(q, kv):
        return _swa(sinks, q, kv, kv, tq=WINDOW, win=2 * WINDOW, sample=False, v_block=1)

    y_p, k_p, v_p, lf_p, kvk_p, kvv_p = _trunk(
        x_prompt, p_prompt, w, tm=512, fox_attend=fox_p, swa_attend=swa_p)

    db, dt, _ = x_sample.shape
    past = cache_fox_k.shape[2]
    ck = cache_fox_k[0].reshape(db, past, D_MODEL)
    cv = cache_fox_v[0].reshape(db, past, D_MODEL)
    cs_tile = 512
    pad = -(past + dt) % cs_tile

    def fox_s(q, kb, vb, lf):
        lf_all = jnp.concatenate(
            [cache_fox_logf[0].astype(F32), lf, jnp.zeros((db, pad, N_HEADS), F32)], axis=1)
        f = _cumsum(lf_all, cs_tile)
        fq4, _ = _pair_layouts(f[:, past:past + dt])
        _, ft4 = _pair_layouts(f[:, :past + dt])
        return _fox_sample(q, kb, vb, ck, cv, fq4, ft4)

    buf = cache_swa_k.shape[1]
    half = SWA_KV_HEADS * HEAD_DIM
    win_s = 2 * WINDOW
    assert buf + dt <= win_s

    def swa_s(q, kv):
        k, v = kv[:, :, :half], kv[:, :, half:]
        front = jnp.zeros((db, win_s - buf - dt, half), BF16)
        k_all = jnp.concatenate([front, cache_swa_k.reshape(db, buf, half).astype(BF16), k], axis=1)
        v_all = jnp.concatenate([front, cache_swa_v.reshape(db, buf, half).astype(BF16), v], axis=1)
        return _swa(sinks, q, k_all, v_all, tq=dt, win=win_s, sample=True, past=past)

    y_s, k_s, v_s, lf_s, kvk_s, kvv_s = _trunk(
        x_sample, p_sample, w, tm=db * dt, fox_attend=fox_s, swa_attend=swa_s)

    buf_p = min(WINDOW, x_prompt.shape[1])
    swa_k_sample = jnp.concatenate([cache_swa_k, kvk_s.astype(cache_swa_k.dtype)], axis=1)[:, -buf:]
    swa_v_sample = jnp.concatenate([cache_swa_v, kvv_s.astype(cache_swa_v.dtype)], axis=1)[:, -buf:]
    return (y_p, y_s, k_p[None], v_p[None], lf_p[None], k_s[None], v_s[None], lf_s[None],
            kvk_p[:, -buf_p:], kvv_p[:, -buf_p:], swa_k_sample, swa_v_sample)
```

```python
import functools

import numpy as np
import jax
import jax.numpy as jnp
from jax import lax
from jax.experimental import pallas as pl
from jax.experimental.pallas import tpu as pltpu

F32 = jnp.float32
BF16 = jnp.bfloat16

D_MODEL = 1024
HEAD_DIM = 64
N_HEADS = D_MODEL // HEAD_DIM
SWA_KV_HEADS = 4
SWA_GROUP = N_HEADS // SWA_KV_HEADS
CHUNK = 64
WIN_CHUNKS = 2
WINDOW = 128
EPS = 1e-6
SCALE = HEAD_DIM ** -0.5
LANES = 128
PAIRS = D_MODEL // LANES
NEG_BIG = -1e30
MASK_DIST = 1e30
VMEM_LIMIT = 56 * 1024 * 1024

SWA_PERM = np.array([8 * c + 4 * hh + g for c in range(2) for g in range(4) for hh in range(2)])
SWA_SLOPES = [float(2.0 ** (-8.0 * (h + 1) / N_HEADS)) for h in SWA_PERM]


def _rms(x, g):
    y = x * lax.rsqrt(jnp.mean(x * x, axis=-1, keepdims=True) + EPS)
    return y * g


def _log_sigmoid(x):
    return jnp.minimum(x, 0.0) - jnp.log1p(jnp.exp(-jnp.abs(x)))


def _dot(a, b):
    return jnp.dot(a, b, preferred_element_type=F32)


def _dot_nt(a, b):
    return lax.dot_general(a, b, (((1,), (1,)), ((), ())), preferred_element_type=F32)


def _row_spec(tm, width):
    return pl.BlockSpec((tm, width), lambda i: (i, 0))


def _const_spec(shape):
    return pl.BlockSpec(shape, lambda *_: (0,) * len(shape))


def _params(sem):
    return pltpu.CompilerParams(dimension_semantics=sem, vmem_limit_bytes=VMEM_LIMIT)


def _fox_in_kernel(x_ref, g_ref, wqkv_ref, wf_ref, bf_ref,
                   q_ref, k_ref, v_ref, kb_ref, vb_ref, lf_ref):
    a = _rms(x_ref[...], g_ref[...]).astype(BF16)
    q = _dot(a, wqkv_ref[:, 0:D_MODEL])
    q_ref[...] = (q * SCALE).astype(BF16)
    k = _dot(a, wqkv_ref[:, D_MODEL:2 * D_MODEL])
    k_ref[...] = k
    kb_ref[...] = k.astype(BF16)
    v = _dot(a, wqkv_ref[:, 2 * D_MODEL:3 * D_MODEL])
    v_ref[...] = v
    vb_ref[...] = v.astype(BF16)
    f = _dot(a, wf_ref[...]) + bf_ref[...]
    lf_ref[...] = _log_sigmoid(f[:, 0:N_HEADS])


def _fox_in(x, g, wqkv, wf, bf, tm):
    n = x.shape[0]
    row = lambda w: _row_spec(tm, w)
    return pl.pallas_call(
        _fox_in_kernel,
        grid=(n // tm,),
        in_specs=[row(D_MODEL), _const_spec((1, D_MODEL)), _const_spec(wqkv.shape),
                  _const_spec(wf.shape), _const_spec(bf.shape)],
        out_specs=[row(D_MODEL)] * 5 + [row(N_HEADS)],
        out_shape=[jax.ShapeDtypeStruct((n, D_MODEL), BF16),
                   jax.ShapeDtypeStruct((n, D_MODEL), F32),
                   jax.ShapeDtypeStruct((n, D_MODEL), F32),
                   jax.ShapeDtypeStruct((n, D_MODEL), BF16),
                   jax.ShapeDtypeStruct((n, D_MODEL), BF16),
                   jax.ShapeDtypeStruct((n, N_HEADS), F32)],
        compiler_params=_params(("parallel",)),
        name="fox_in",
    )(x, g, wqkv, wf, bf)


AUG_NEG = HEAD_DIM
AUG_POS = HEAD_DIM + 3
AUG_END = HEAD_DIM + 6


def _placement():
    pmat = np.zeros((LANES, N_HEADS * LANES), np.float32)
    for part in range(3):
        for h in range(N_HEADS):
            pmat[part * N_HEADS + h, h * LANES + AUG_NEG + part] = 1.0
            pmat[part * N_HEADS + h, h * LANES + AUG_POS + part] = 1.0
    return pmat


def _fox_in_prompt_kernel(x_ref, g_ref, wqkv_ref, wf_ref, bf_ref, place_ref,
                          qa_ref, ka_ref, vt_ref, k_ref, v_ref, lf_ref, carry_ref, *, tiles_per_seq):
    tm = x_ref.shape[0]

    @pl.when(pl.program_id(0) % tiles_per_seq == 0)
    def _():
        carry_ref[...] = jnp.zeros_like(carry_ref)

    a = _rms(x_ref[...], g_ref[...]).astype(BF16)
    lf = _log_sigmoid(_dot(a, wf_ref[...]) + bf_ref[...])
    lf_ref[...] = lf[:, 0:N_HEADS]

    r = lax.broadcasted_iota(jnp.int32, (tm, tm), 0)
    c = lax.broadcasted_iota(jnp.int32, (tm, tm), 1)
    tril = jnp.where(c <= r, 1.0, 0.0).astype(F32)
    cs = jnp.dot(tril, lf, preferred_element_type=F32,
                 precision=lax.Precision.HIGHEST) + carry_ref[...]
    carry_ref[...] = cs[tm - 1:tm, :]

    hi = cs.astype(BF16).astype(F32)
    r1 = cs - hi
    mid = r1.astype(BF16).astype(F32)
    lo = (r1 - mid).astype(BF16).astype(F32)
    lane = lax.broadcasted_iota(jnp.int32, (tm, LANES), 1)
    fcat = jnp.where(lane < N_HEADS, hi,
                     jnp.where(lane < 2 * N_HEADS, pltpu.roll(mid, N_HEADS, 1),
                               jnp.where(lane < 3 * N_HEADS, pltpu.roll(lo, 2 * N_HEADS, 1), 0.0)))
    aug = _dot(fcat.astype(BF16), place_ref[...])

    q = _dot(a, wqkv_ref[:, 0:D_MODEL]) * SCALE
    k = _dot(a, wqkv_ref[:, D_MODEL:2 * D_MODEL])
    v = _dot(a, wqkv_ref[:, 2 * D_MODEL:3 * D_MODEL])
    k_ref[...] = k
    v_ref[...] = v
    vt_ref[0] = v.T.astype(BF16)

    data = lane < HEAD_DIM
    q_tail = jnp.where(lane < AUG_POS, -1.0, 0.0)
    k_tail = jnp.where((lane >= AUG_POS) & (lane < AUG_END), 1.0, 0.0)
    in_neg = lane < AUG_POS
    for pair in range(PAIRS):
        q2 = q[:, pair * LANES:(pair + 1) * LANES]
        k2 = k[:, pair * LANES:(pair + 1) * LANES]
        srcs = ((q2, k2), (pltpu.roll(q2, HEAD_DIM, 1), pltpu.roll(k2, HEAD_DIM, 1)))
        for hh in range(2):
            h = 2 * pair + hh
            qs, ks = srcs[hh]
            g = aug[:, h * LANES:(h + 1) * LANES]
            qa = jnp.where(data, qs, jnp.where(in_neg, q_tail, g))
            ka = jnp.where(data, ks, jnp.where(in_neg, g, k_tail))
            qa_ref[0, h] = qa.astype(BF16)
            ka_ref[0, h] = ka.astype(BF16)


def _fox_in_prompt(x, g, wqkv, wf, bf, place, b, s, tm):
    n = x.shape[0]
    nt = s // tm
    row = lambda w: _row_spec(tm, w)
    head_spec = pl.BlockSpec((1, N_HEADS, tm, LANES), lambda i: (i // nt, 0, i % nt, 0))
    return pl.pallas_call(
        functools.partial(_fox_in_prompt_kernel, tiles_per_seq=nt),
        grid=(n // tm,),
        in_specs=[row(D_MODEL), _const_spec((1, D_MODEL)), _const_spec(wqkv.shape),
                  _const_spec(wf.shape), _const_spec(bf.shape), _const_spec(place.shape)],
        out_specs=[head_spec, head_spec,
                   pl.BlockSpec((1, D_MODEL, tm), lambda i: (i // nt, 0, i % nt)),
                   row(D_MODEL), row(D_MODEL), row(N_HEADS)],
        out_shape=[jax.ShapeDtypeStruct((b, N_HEADS, s, LANES), BF16),
                   jax.ShapeDtypeStruct((b, N_HEADS, s, LANES), BF16),
                   jax.ShapeDtypeStruct((b, D_MODEL, s), BF16),
                   jax.ShapeDtypeStruct((n, D_MODEL), F32),
                   jax.ShapeDtypeStruct((n, D_MODEL), F32),
                   jax.ShapeDtypeStruct((n, N_HEADS), F32)],
        scratch_shapes=[pltpu.VMEM((1, LANES), F32)],
        compiler_params=_params(("arbitrary",)),
        name="fox_in_prompt",
    )(x, g, wqkv, wf, bf, place)


def _cumsum_kernel(lf_ref, f_ref, carry_ref):
    ts = lf_ref.shape[1]

    @pl.when(pl.program_id(1) == 0)
    def _():
        carry_ref[...] = jnp.zeros_like(carry_ref)

    r = lax.broadcasted_iota(jnp.int32, (ts, ts), 0)
    c = lax.broadcasted_iota(jnp.int32, (ts, ts), 1)
    tril = jnp.where(c <= r, 1.0, 0.0).astype(F32)
    cs = jnp.dot(tril, lf_ref[0], preferred_element_type=F32,
                 precision=lax.Precision.HIGHEST) + carry_ref[...]
    f_ref[0] = cs
    carry_ref[...] = cs[ts - 1:ts, :]


def _cumsum(lf, ts):
    b, s, h = lf.shape
    spec = pl.BlockSpec((1, ts, h), lambda bi, i: (bi, i, 0))
    return pl.pallas_call(
        _cumsum_kernel,
        grid=(b, s // ts),
        in_specs=[spec],
        out_specs=spec,
        out_shape=jax.ShapeDtypeStruct((b, s, h), F32),
        scratch_shapes=[pltpu.VMEM((1, h), F32)],
        compiler_params=_params(("parallel", "arbitrary")),
        name="logf_cumsum",
    )(lf)


def _pair_layouts(f):
    b, l, _ = f.shape
    f4 = f.reshape(b, l, PAIRS, 2)
    return jnp.transpose(f4, (0, 2, 1, 3)), jnp.transpose(f4, (0, 2, 3, 1))


def _head_masks(shape):
    lane = lax.broadcasted_iota(jnp.int32, shape, 1)
    return lane < HEAD_DIM


def _fox_kernel(qa_ref, ka_ref, vt_ref, o_ref, m_ref, l_ref, acc_ref, *, tq, tk):
    i = pl.program_id(2)
    per_chunk = tk // tq
    n_full = lax.div(i, per_chunk)

    def scores(h, start, size):
        return _dot_nt(ka_ref[0, h, pl.ds(start, size), :], qa_ref[0, h])

    def values(h, start, size):
        return vt_ref[0, h * HEAD_DIM:(h + 1) * HEAD_DIM, pl.ds(start, size)]

    d0 = pl.multiple_of(i * tq, tq)
    key = lax.broadcasted_iota(jnp.int32, (tq, tq), 0)
    qry = lax.broadcasted_iota(jnp.int32, (tq, tq), 1)
    causal = key <= qry
    sd = [jnp.where(causal, scores(h, d0, tq), NEG_BIG) for h in range(2)]
    for h in range(2):
        s = sd[h]
        m = jnp.max(s, axis=0, keepdims=True)
        p = jnp.exp(s - m)
        m_ref[h] = m
        l_ref[h] = jnp.sum(p, axis=0, keepdims=True)
        acc_ref[h] = _dot(values(h, d0, tq), p.astype(BF16))

    def update(start, size):
        sj = [scores(h, start, size) for h in range(2)]
        for h in range(2):
            s = sj[h]
            m_prev = m_ref[h]
            m_new = jnp.maximum(m_prev, jnp.max(s, axis=0, keepdims=True))
            alpha = jnp.exp(m_prev - m_new)
            p = jnp.exp(s - m_new)
            m_ref[h] = m_new
            l_ref[h] = alpha * l_ref[h] + jnp.sum(p, axis=0, keepdims=True)
            acc_ref[h] = alpha * acc_ref[h] + _dot(values(h, start, size), p.astype(BF16))

    def body(j, carry):
        update(pl.multiple_of(j * tk, tk), tk)
        return carry

    lax.fori_loop(0, n_full, body, 0)

    for extra in range(per_chunk - 1):
        @pl.when(i - n_full * per_chunk > extra)
        def _():
            update(pl.multiple_of((n_full * per_chunk + extra) * tq, tq), tq)

    ot = jnp.concatenate([acc_ref[0] / l_ref[0], acc_ref[1] / l_ref[1]], axis=0)
    o_ref[0] = ot.T.astype(o_ref.dtype)


def _fox_prompt(qa, ka, vt, tq, tk):
    b, _, s, _ = qa.shape
    assert s % tk == 0 and tk % tq == 0
    return pl.pallas_call(
        functools.partial(_fox_kernel, tq=tq, tk=tk),
        grid=(b, PAIRS, s // tq),
        in_specs=[pl.BlockSpec((1, 2, tq, LANES), lambda bi, hp, i: (bi, hp, i, 0)),
                  pl.BlockSpec((1, 2, s, LANES), lambda bi, hp, i: (bi, hp, 0, 0)),
                  pl.BlockSpec((1, LANES, s), lambda bi, hp, i: (bi, hp, 0))],
        out_specs=pl.BlockSpec((1, tq, LANES), lambda bi, hp, i: (bi, i, hp)),
        out_shape=jax.ShapeDtypeStruct((b, s, D_MODEL), BF16),
        scratch_shapes=[pltpu.VMEM((2, 1, tq), F32), pltpu.VMEM((2, 1, tq), F32),
                        pltpu.VMEM((2, HEAD_DIM, tq), F32)],
        compiler_params=_params(("parallel", "parallel", "arbitrary")),
        name="fox_attention",
    )(qa, ka, vt)


def _fox_sample_kernel(q_ref, kn_ref, vn_ref, ck_ref, cv_ref, fq_ref, ft_ref, o_ref):
    t = q_ref.shape[1]
    past = ck_ref.shape[1]
    q = q_ref[0]
    first = _head_masks(q.shape)
    zero = jnp.zeros_like(q)
    qh = (jnp.where(first, q, zero), jnp.where(first, zero, q))
    ck = ck_ref[0].astype(BF16)
    cv = cv_ref[0].astype(BF16)
    kn = kn_ref[0]
    vn = vn_ref[0]
    r = lax.broadcasted_iota(jnp.int32, (t, t), 0)
    c = lax.broadcasted_iota(jnp.int32, (t, t), 1)
    outs = []
    for h in range(2):
        fq = fq_ref[0, 0, :, h:h + 1]
        s_c = _dot_nt(qh[h], ck) + fq - ft_ref[0, 0, h:h + 1, 0:past]
        s_n = _dot_nt(qh[h], kn) + fq - ft_ref[0, 0, h:h + 1, past:past + t]
        s_n = jnp.where(c <= r, s_n, NEG_BIG)
        m = jnp.maximum(jnp.max(s_c, axis=1, keepdims=True), jnp.max(s_n, axis=1, keepdims=True))
        p_c = jnp.exp(s_c - m)
        p_n = jnp.exp(s_n - m)
        l = jnp.sum(p_c, axis=1, keepdims=True) + jnp.sum(p_n, axis=1, keepdims=True)
        o = _dot(p_c.astype(BF16), cv) + _dot(p_n.astype(BF16), vn)
        outs.append(o / l)
    o_ref[0] = jnp.where(first, outs[0], outs[1]).astype(o_ref.dtype)


def _fox_sample(q, kb, vb, ck, cv, fq4, ft4):
    b, t, _ = q.shape
    past = ck.shape[1]
    new_spec = pl.BlockSpec((1, t, LANES), lambda bi, hp: (bi, 0, hp))
    cache_spec = pl.BlockSpec((1, past, LANES), lambda bi, hp: (bi, 0, hp))
    return pl.pallas_call(
        _fox_sample_kernel,
        grid=(b, PAIRS),
        in_specs=[new_spec, new_spec, new_spec, cache_spec, cache_spec,
                  pl.BlockSpec((1, 1, t, 2), lambda bi, hp: (bi, hp, 0, 0)),
                  pl.BlockSpec((1, 1, 2, past + t), lambda bi, hp: (bi, hp, 0, 0))],
        out_specs=new_spec,
        out_shape=jax.ShapeDtypeStruct((b, t, D_MODEL), BF16),
        compiler_params=_params(("parallel", "parallel")),
        name="fox_attention_sample",
    )(q, kb, vb, ck, cv, fq4, ft4)


def _swa_kernel(sink_ref, q_ref, k_ref, v_ref, o_ref, *, tq, win, sample, past, slopes):
    if sample:
        kstart = 0
        qpos0 = past
        kpos0 = past + tq - win
    else:
        i = pl.program_id(1)
        kstart = pl.multiple_of(jnp.maximum(i * tq - WINDOW, 0), WINDOW)
        qpos0 = i * tq
        kpos0 = kstart
    qpos = qpos0 + lax.broadcasted_iota(jnp.int32, (tq, win), 0)
    kpos = kpos0 + lax.broadcasted_iota(jnp.int32, (tq, win), 1)
    dc = lax.shift_right_arithmetic(qpos, 6) - lax.shift_right_arithmetic(kpos, 6)
    dist = jnp.abs(qpos - kpos).astype(F32)
    dist = jnp.where(dc >= 0, dist, MASK_DIST)
    dist = jnp.where(dc <= WIN_CHUNKS, dist, MASK_DIST)
    dist = jnp.where(kpos >= 0, dist, MASK_DIST)
    first = _head_masks((tq, LANES))
    for c in range(2):
        kblk = k_ref[0, pl.ds(kstart, win), c * LANES:(c + 1) * LANES]
        vblk = v_ref[0, pl.ds(kstart, win), c * LANES:(c + 1) * LANES]
        for g in range(SWA_GROUP):
            pair = SWA_GROUP * c + g
            q = q_ref[0, :, pair * LANES:(pair + 1) * LANES]
            zero = jnp.zeros_like(q)
            outs = []
            for hh in range(2):
                head = 2 * pair + hh
                qm = jnp.where(first, q, zero) if hh == 0 else jnp.where(first, zero, q)
                s = _dot_nt(qm, kblk) - slopes[head] * dist
                sink = sink_ref[head]
                m = jnp.maximum(jnp.max(s, axis=1, keepdims=True), sink)
                e = jnp.exp(s - m)
                den = jnp.sum(e, axis=1, keepdims=True) + jnp.exp(sink - m)
                outs.append(_dot(e.astype(BF16), vblk) / den)
            o_ref[0, :, pair * LANES:(pair + 1) * LANES] = (
                jnp.where(first, outs[0], outs[1]).astype(o_ref.dtype))


def _swa(sinks, q, k, v, tq, win, sample, past=0, v_block=0):
    b, s, _ = q.shape
    ks = k.shape[1]
    q_spec = pl.BlockSpec((1, tq, D_MODEL), lambda bi, i: (bi, i, 0))
    k_spec = pl.BlockSpec((1, ks, 2 * LANES), lambda bi, i: (bi, 0, 0))
    v_spec = pl.BlockSpec((1, ks, 2 * LANES), lambda bi, i: (bi, 0, v_block))
    return pl.pallas_call(
        functools.partial(_swa_kernel, tq=tq, win=win, sample=sample, past=past,
                          slopes=SWA_SLOPES),
        grid=(b, s // tq),
        in_specs=[pl.BlockSpec(memory_space=pltpu.SMEM), q_spec, k_spec, v_spec],
        out_specs=q_spec,
        out_shape=jax.ShapeDtypeStruct((b, s, D_MODEL), BF16),
        compiler_params=_params(("parallel", "arbitrary")),
        name="swa_attention_sample" if sample else "swa_attention",
    )(sinks, q, k, v)


def _mix_out_kernel(o_ref, h_ref, w_ref, g_ref, out_ref):
    mix = _dot(o_ref[...], w_ref[...])
    out_ref[...] = h_ref[...] + _rms(mix, g_ref[...])


def _mix_out(o, h, w, g, tm):
    n = h.shape[0]
    return pl.pallas_call(
        _mix_out_kernel,
        grid=(n // tm,),
        in_specs=[_row_spec(tm, D_MODEL), _row_spec(tm, D_MODEL), _const_spec(w.shape),
                  _const_spec((1, D_MODEL))],
        out_specs=_row_spec(tm, D_MODEL),
        out_shape=jax.ShapeDtypeStruct((n, D_MODEL), F32),
        compiler_params=_params(("parallel",)),
        name="mix_out",
    )(o, h, w, g)


def _ffn_kernel(h_ref, gpre_ref, wg_ref, wu_ref, wd_ref, gpost_ref, out_ref):
    h = h_ref[...]
    a = _rms(h, gpre_ref[...]).astype(BF16)
    gate = _dot(a, wg_ref[...])
    up = _dot(a, wu_ref[...])
    act = (gate * jax.nn.sigmoid(gate) * up).astype(BF16)
    f = _dot(act, wd_ref[...])
    out_ref[...] = h + _rms(f, gpost_ref[...])


def _ffn(h, gpre, wg, wu, wd, gpost, tm):
    n = h.shape[0]
    vec = _const_spec((1, D_MODEL))
    return pl.pallas_call(
        _ffn_kernel,
        grid=(n // tm,),
        in_specs=[_row_spec(tm, D_MODEL), vec, _const_spec(wg.shape), _const_spec(wu.shape),
                  _const_spec(wd.shape), vec],
        out_specs=_row_spec(tm, D_MODEL),
        out_shape=jax.ShapeDtypeStruct((n, D_MODEL), F32),
        compiler_params=_params(("parallel",)),
        name="ffn",
    )(h, gpre, wg, wu, wd, gpost)


def _ple_update(h_ref, p_ref, gn_ref, wgate_ref, bgate_ref, wproj_ref):
    h = h_ref[...]
    gate = jax.nn.sigmoid(_dot(_rms(h, gn_ref[...]).astype(BF16), wgate_ref[...]) + bgate_ref[...])
    emb = _dot(p_ref[...].astype(BF16), wproj_ref[...])
    return h + gate * emb


def _ple_kernel(h_ref, p_ref, gn_ref, wgate_ref, bgate_ref, wproj_ref, out_ref):
    out_ref[...] = _ple_update(h_ref, p_ref, gn_ref, wgate_ref, bgate_ref, wproj_ref)


def _ple_kv_q_kernel(h_ref, p_ref, gn_ref, wgate_ref, bgate_ref, wproj_ref,
                     gkv_ref, wkv_ref, gq_ref, wq_ref, out_ref, kv_ref, kvb_ref, q_ref):
    h = _ple_update(h_ref, p_ref, gn_ref, wgate_ref, bgate_ref, wproj_ref)
    out_ref[...] = h
    kv = _dot(_rms(h, gkv_ref[...]).astype(BF16), wkv_ref[...])
    kv_ref[...] = kv
    kvb_ref[...] = kv.astype(BF16)
    q = _dot(_rms(h, gq_ref[...]).astype(BF16), wq_ref[...])
    q_ref[...] = (q * SCALE).astype(BF16)


def _ple(h, p, gn, wgate, bgate, wproj, tm, extra=None):
    n = h.shape[0]
    vec = _const_spec((1, D_MODEL))
    in_specs = [_row_spec(tm, D_MODEL), _row_spec(tm, p.shape[1]), vec,
                _const_spec(wgate.shape), vec, _const_spec(wproj.shape)]
    args = [h, p, gn, wgate, bgate, wproj]
    if extra is None:
        return pl.pallas_call(
            _ple_kernel,
            grid=(n // tm,),
            in_specs=in_specs,
            out_specs=_row_spec(tm, D_MODEL),
            out_shape=jax.ShapeDtypeStruct((n, D_MODEL), F32),
            compiler_params=_params(("parallel",)),
            name="ple",
        )(*args)
    gkv, wkv, gq, wq = extra
    kvw = wkv.shape[1]
    return pl.pallas_call(
        _ple_kv_q_kernel,
        grid=(n // tm,),
        in_specs=in_specs + [vec, _const_spec(wkv.shape), vec, _const_spec(wq.shape)],
        out_specs=[_row_spec(tm, D_MODEL), _row_spec(tm, kvw), _row_spec(tm, kvw),
                   _row_spec(tm, D_MODEL)],
        out_shape=[jax.ShapeDtypeStruct((n, D_MODEL), F32),
                   jax.ShapeDtypeStruct((n, kvw), F32),
                   jax.ShapeDtypeStruct((n, kvw), BF16),
                   jax.ShapeDtypeStruct((n, D_MODEL), BF16)],
        compiler_params=_params(("parallel",)),
        name="ple_kv_q",
    )(*args, gkv, wkv, gq, wq)


def _trunk(x, p, w, *, tm, fox_mixer, swa_attend):
    b, t, _ = x.shape
    n = b * t
    h = x.reshape(n, D_MODEL)
    p = p.reshape(2, n, -1)

    o, k, v, lf = fox_mixer(h)
    h = _mix_out(o.reshape(n, D_MODEL), h, w["fox_wout"], w["g_post0"], tm)
    h = _ffn(h, w["g_fpre0"], w["wg0"], w["wu0"], w["wd0"], w["g_fpost0"], tm)
    h, kv, kvb, q1 = _ple(h, p[0], w["g_ple0"], w["wpg0"], w["bpg0"], w["wpp0"], tm,
                          extra=(w["g_kv"], w["wkv"], w["g_pre1"], w["swa_wq"]))

    o = swa_attend(q1.reshape(b, t, D_MODEL), kvb.reshape(b, t, -1))
    h = _mix_out(o.reshape(n, D_MODEL), h, w["swa_wout"], w["g_post1"], tm)
    h = _ffn(h, w["g_fpre1"], w["wg1"], w["wu1"], w["wd1"], w["g_fpost1"], tm)
    h = _ple(h, p[1], w["g_ple1"], w["wpg1"], w["bpg1"], w["wpp1"], tm)

    kv4 = kv.reshape(b, t, 2, SWA_KV_HEADS, HEAD_DIM)
    return (h.reshape(b, t, D_MODEL), k.reshape(b, t, N_HEADS, HEAD_DIM),
            v.reshape(b, t, N_HEADS, HEAD_DIM), lf.reshape(b, t, N_HEADS),
            kv4[:, :, 0], kv4[:, :, 1])


def kernel(x_prompt, x_sample, cache_fox_k, cache_fox_v, cache_fox_logf, cache_swa_k,
           cache_swa_v, p_prompt, p_sample, norm_mix_pre, norm_mix_post, norm_ffn_pre,
           norm_ffn_post, fox_w_in, fox_b_f, fox_w_out, swa_w_q, swa_sinks, swa_w_out,
           kv_norm, swa_w_kv, ffn_w_gate, ffn_w_up, ffn_w_down, ple_norm, ple_w_gate,
           ple_b_gate, ple_w_proj):
    assert norm_mix_pre.shape[0] == 2 and fox_w_in.shape[0] == 1 and swa_w_q.shape[0] == 1
    vec = lambda a: a.reshape(1, -1).astype(F32)
    nqkv = 3 * D_MODEL
    wf = jnp.zeros((D_MODEL, LANES), F32).at[:, :N_HEADS].set(fox_w_in[0][:, nqkv:])
    bf = jnp.zeros((1, LANES), F32).at[:, :N_HEADS].set(fox_b_f[0][None, :])
    wq = swa_w_q[0].reshape(D_MODEL, N_HEADS, HEAD_DIM)[:, SWA_PERM].reshape(D_MODEL, D_MODEL)
    wo = swa_w_out[0].reshape(N_HEADS, HEAD_DIM, D_MODEL)[SWA_PERM].reshape(D_MODEL, D_MODEL)
    sinks = swa_sinks[0][SWA_PERM].astype(F32)
    w = {
        "g_pre0": vec(norm_mix_pre[0]), "g_pre1": vec(norm_mix_pre[1]),
        "g_post0": vec(norm_mix_post[0]), "g_post1": vec(norm_mix_post[1]),
        "g_fpre0": vec(norm_ffn_pre[0]), "g_fpre1": vec(norm_ffn_pre[1]),
        "g_fpost0": vec(norm_ffn_post[0]), "g_fpost1": vec(norm_ffn_post[1]),
        "g_ple0": vec(ple_norm[0]), "g_ple1": vec(ple_norm[1]), "g_kv": vec(kv_norm),
        "wqkv": fox_w_in[0][:, :nqkv].astype(BF16), "wf": wf.astype(BF16), "bf": bf,
        "fox_wout": fox_w_out[0].astype(BF16),
        "swa_wq": wq.astype(BF16), "swa_wout": wo.astype(BF16),
        "wkv": swa_w_kv.astype(BF16),
        "wg0": ffn_w_gate[0].astype(BF16), "wg1": ffn_w_gate[1].astype(BF16),
        "wu0": ffn_w_up[0].astype(BF16), "wu1": ffn_w_up[1].astype(BF16),
        "wd0": ffn_w_down[0].astype(BF16), "wd1": ffn_w_down[1].astype(BF16),
        "wpg0": ple_w_gate[0].astype(BF16), "wpg1": ple_w_gate[1].astype(BF16),
        "bpg0": vec(ple_b_gate[0]), "bpg1": vec(ple_b_gate[1]),
        "wpp0": ple_w_proj[0].astype(BF16), "wpp1": ple_w_proj[1].astype(BF16),
    }

    pb, ps, _ = x_prompt.shape
    place = jnp.asarray(_placement(), BF16)

    def fox_p(h):
        qa, ka, vt, k, v, lf = _fox_in_prompt(h, w["g_pre0"], w["wqkv"], w["wf"], w["bf"], place,
                                              pb, ps, 256)
        return _fox_prompt(qa, ka, vt, 512, 1024), k, v, lf

    def swa_p(q, kv):
        return _swa(sinks, q, kv, kv, tq=WINDOW, win=2 * WINDOW, sample=False, v_block=1)

    y_p, k_p, v_p, lf_p, kvk_p, kvv_p = _trunk(
        x_prompt, p_prompt, w, tm=512, fox_mixer=fox_p, swa_attend=swa_p)

    db, dt, _ = x_sample.shape
    past = cache_fox_k.shape[2]
    ck = cache_fox_k[0].reshape(db, past, D_MODEL)
    cv = cache_fox_v[0].reshape(db, past, D_MODEL)
    cs_tile = 512
    pad = -(past + dt) % cs_tile

    def fox_s(h):
        q, k, v, kb, vb, lf = _fox_in(h, w["g_pre0"], w["wqkv"], w["wf"], w["bf"], db * dt)
        lf_all = jnp.concatenate(
            [cache_fox_logf[0].astype(F32), lf.reshape(db, dt, N_HEADS),
             jnp.zeros((db, pad, N_HEADS), F32)], axis=1)
        f = _cumsum(lf_all, cs_tile)
        fq4, _ = _pair_layouts(f[:, past:past + dt])
        _, ft4 = _pair_layouts(f[:, :past + dt])
        o = _fox_sample(q.reshape(db, dt, D_MODEL), kb.reshape(db, dt, D_MODEL),
                        vb.reshape(db, dt, D_MODEL), ck, cv, fq4, ft4)
        return o, k, v, lf

    buf = cache_swa_k.shape[1]
    half = SWA_KV_HEADS * HEAD_DIM
    win_s = 2 * WINDOW
    assert buf + dt <= win_s

    def swa_s(q, kv):
        k, v = kv[:, :, :half], kv[:, :, half:]
        front = jnp.zeros((db, win_s - buf - dt, half), BF16)
        k_all = jnp.concatenate([front, cache_swa_k.reshape(db, buf, half).astype(BF16), k], axis=1)
        v_all = jnp.concatenate([front, cache_swa_v.reshape(db, buf, half).astype(BF16), v], axis=1)
        return _swa(sinks, q, k_all, v_all, tq=dt, win=win_s, sample=True, past=past)

    y_s, k_s, v_s, lf_s, kvk_s, kvv_s = _trunk(
        x_sample, p_sample, w, tm=db * dt, fox_mixer=fox_s, swa_attend=swa_s)

    buf_p = min(WINDOW, x_prompt.shape[1])
    swa_k_sample = jnp.concatenate([cache_swa_k, kvk_s.astype(cache_swa_k.dtype)], axis=1)[:, -buf:]
    swa_v_sample = jnp.concatenate([cache_swa_v, kvv_s.astype(cache_swa_v.dtype)], axis=1)[:, -buf:]
    return (y_p, y_s, k_p[None], v_p[None], lf_p[None], k_s[None], v_s[None], lf_s[None],
            kvk_p[:, -buf_p:], kvv_p[:, -buf_p:], swa_k_sample, swa_v_sample)
```
